```python
import math
import jax, jax.numpy as jnp
from jax import lax
import numpy as np

D_MODEL = 2048
BATCH = 16
SEQ = 2048
DEPTH = 4
DEC_BATCH = 32
DEC_SEQ = 32
PAST_LEN = 4096

CHUNK = 64
N_META = 16
N_A_LAYERS = DEPTH // 2
N_B_LAYERS = DEPTH - N_A_LAYERS
DN_HEAD_DIM = 128
DN_QK_HEADS = D_MODEL // 128
DN_V_HEADS = 2 * DN_QK_HEADS
DN_QK_W = DN_QK_HEADS * DN_HEAD_DIM
DN_V_W = DN_V_HEADS * DN_HEAD_DIM
CONV_W = 4
CONV_DIM = 2 * DN_QK_W + DN_V_W
DN_IN = CONV_DIM + DN_V_W + 2 * DN_V_HEADS
FX_HEAD_DIM = 128
FX_HEADS = D_MODEL // FX_HEAD_DIM
FX_W = FX_HEADS * FX_HEAD_DIM
FX_Q_BLOCK = 128
KV_IN = 2 * FX_W + FX_HEADS
EPS = 1e-6

kernel_name = 'yoco_deltanet_fox_stream_step'


def rms_norm(x, g):
    xf = x.astype(jnp.float32)
    y = xf * lax.rsqrt(jnp.mean(xf * xf, axis=-1, keepdims=True) + EPS)
    return (y * g.astype(jnp.float32)).astype(x.dtype)


def l2_norm(x):
    xf = x.astype(jnp.float32)
    return xf * lax.rsqrt(jnp.sum(xf * xf, axis=-1, keepdims=True) + EPS)


def short_conv(xp, w):
    return lax.conv_general_dilated(
        xp, w[:, None, :].astype(xp.dtype), window_strides=(1,), padding='VALID',
        dimension_numbers=('NWC', 'WIO', 'NWC'), feature_group_count=xp.shape[-1])


def gated_delta_rule(q, k, v, g, beta, s0):
    B, T, HK, DK = q.shape
    HV, DV = v.shape[2], v.shape[3]
    rep = HV // HK
    n = -(-T // CHUNK)
    pad = n * CHUNK - T

    def blocks(a):
        a = jnp.pad(a, [(0, 0), (0, pad)] + [(0, 0)] * (a.ndim - 2))
        a = a.reshape((B, n, CHUNK) + a.shape[2:])
        return jnp.swapaxes(jnp.moveaxis(a, 1, 0), 2, 3)

    idx = jnp.arange(CHUNK)
    incl = idx[:, None] >= idx[None, :]
    strict = idx[:, None] > idx[None, :]
    eye = jnp.eye(CHUNK, dtype=jnp.float32)

    def step(S, inp):
        qc, kc, vc, gc, bc = inp
        qc = jnp.repeat(qc, rep, axis=1)
        kc = jnp.repeat(kc, rep, axis=1)
        G = jnp.cumsum(gc, axis=-1)
        decay = jnp.exp(jnp.where(incl, G[..., :, None] - G[..., None, :], -jnp.inf))
        kk = jnp.einsum('bhik,bhjk->bhij', kc, kc)
        m = jnp.where(strict, bc[..., :, None] * kk * decay, 0.0)
        rhs = jnp.concatenate([vc * bc[..., None], kc * (bc * jnp.exp(G))[..., None]], axis=-1)
        sol = lax.linalg.triangular_solve(m + eye, rhs, left_side=True, lower=True,
                                          unit_diagonal=True)
        u, w = sol[..., :DV], sol[..., DV:]
        v_new = u - jnp.einsum('bhck,bhkv->bhcv', w, S)
        qk = jnp.einsum('bhik,bhjk->bhij', qc, kc) * decay
        o = (jnp.einsum('bhck,bhkv->bhcv', qc * jnp.exp(G)[..., None], S)
             + jnp.einsum('bhij,bhjv->bhiv', qk, v_new))
        g_last = G[..., -1]
        S = (S * jnp.exp(g_last)[..., None, None]
             + jnp.einsum('bhck,bhcv->bhkv', kc * jnp.exp(g_last[..., None] - G)[..., None], v_new))
        return S, o

    s_fin, o = lax.scan(step, s0, (blocks(q), blocks(k), blocks(v), blocks(g), blocks(beta)))
    o = jnp.moveaxis(jnp.swapaxes(o, 2, 3), 0, 1).reshape(B, n * CHUNK, HV, DV)[:, :T]
    return o, s_fin


def delta_layer(h, conv_prev, s0, norm_g, w_in, conv_w, a_log, dt_bias, out_norm_g, w_out):
    f32 = jnp.float32
    B, L, _ = h.shape
    u = rms_norm(h, norm_g) @ w_in
    qkv = u[..., :CONV_DIM]
    z = u[..., CONV_DIM:CONV_DIM + DN_V_W]
    b = u[..., CONV_DIM + DN_V_W:CONV_DIM + DN_V_W + DN_V_HEADS]
    a = u[..., CONV_DIM + DN_V_W + DN_V_HEADS:]
    xp = jnp.concatenate([conv_prev.astype(qkv.dtype), qkv], axis=1)
    conv_state = xp[:, xp.shape[1] - (CONV_W - 1):]
    c = jax.nn.silu(short_conv(xp, conv_w))
    q = l2_norm(c[..., :DN_QK_W].reshape(B, L, DN_QK_HEADS, DN_HEAD_DIM)) * (DN_HEAD_DIM ** -0.5)
    k = l2_norm(c[..., DN_QK_W:2 * DN_QK_W].reshape(B, L, DN_QK_HEADS, DN_HEAD_DIM))
    v = c[..., 2 * DN_QK_W:].reshape(B, L, DN_V_HEADS, DN_HEAD_DIM).astype(f32)
    beta = jax.nn.sigmoid(b.astype(f32))
    g = -jnp.exp(a_log.astype(f32)) * jax.nn.softplus(a.astype(f32) + dt_bias.astype(f32))
    o, s_new = gated_delta_rule(q, k, v, g, beta, s0)
    o = rms_norm(o, out_norm_g) * jax.nn.silu(z.astype(f32)).reshape(B, L, DN_V_HEADS, DN_HEAD_DIM)
    return h + o.reshape(B, L, DN_V_W).astype(h.dtype) @ w_out, conv_state, s_new


def kv_side(h, kv_norm_g, kv_w, f_bias, k_norm_g):
    B, L, _ = h.shape
    u = rms_norm(h, kv_norm_g) @ kv_w
    k = rms_norm(u[..., :FX_W].reshape(B, L, FX_HEADS, FX_HEAD_DIM), k_norm_g)
    v = u[..., FX_W:2 * FX_W].reshape(B, L, FX_HEADS, FX_HEAD_DIM)
    logf = jax.nn.log_sigmoid(u[..., 2 * FX_W:].astype(jnp.float32) + f_bias.astype(jnp.float32))
    return k, v, logf


def fox_queries(h, norm_g, w_in, q_norm_g):
    B, L, _ = h.shape
    u = rms_norm(h, norm_g) @ w_in
    q = rms_norm(u[..., :FX_W].reshape(B, L, FX_HEADS, FX_HEAD_DIM), q_norm_g)
    return q, u[..., FX_W:]


def fox_attend(q, k, v, fq, fk, q_pos, k_pos):
    s = jnp.einsum('bqhd,bkhd->bhqk', q, k).astype(jnp.float32) * (FX_HEAD_DIM ** -0.5)
    s = s + jnp.swapaxes(fq, 1, 2)[..., :, None] - jnp.swapaxes(fk, 1, 2)[..., None, :]
    s = jnp.where(k_pos[None, :] <= q_pos[:, None], s, -jnp.inf)
    p = jax.nn.softmax(s, axis=-1)
    return jnp.einsum('bhqk,bkhd->bqhd', p.astype(v.dtype), v)


def fox_prompt_attend(q, k, v, f):
    B, L, H, d = q.shape
    nb = -(-L // FX_Q_BLOCK)
    pad = nb * FX_Q_BLOCK - L
    qb = jnp.swapaxes(jnp.pad(q, ((0, 0), (0, pad), (0, 0), (0, 0))).reshape(B, nb, FX_Q_BLOCK, H, d), 0, 1)
    fb = jnp.swapaxes(jnp.pad(f, ((0, 0), (0, pad), (0, 0))).reshape(B, nb, FX_Q_BLOCK, H), 0, 1)
    pb = jnp.arange(nb * FX_Q_BLOCK).reshape(nb, FX_Q_BLOCK)
    k_pos = jnp.arange(L)
    o = lax.map(lambda a: fox_attend(a[0], k, v, a[1], f, a[2], k_pos), (qb, fb, pb))
    return jnp.swapaxes(o, 0, 1).reshape(B, nb * FX_Q_BLOCK, H, d)[:, :L]


def fox_out(h, o, z, w_out):
    B, L, _ = h.shape
    return h + (o.reshape(B, L, FX_W) * jax.nn.silu(z)).astype(h.dtype) @ w_out


def setup_inputs(seed: int = 0) -> dict:
    key = jax.random.key(seed)
    ks = jax.random.split(key, 24)
    f32 = jnp.float32

    def nrm(k, shape, scale):
        return jax.random.normal(k, shape, f32) * scale

    dt = jnp.exp(jax.random.uniform(ks[12], (N_A_LAYERS, DN_V_HEADS), f32,
                                    math.log(1e-3), math.log(1e-1)))
    return {
        'x_prompt': nrm(ks[0], (BATCH, SEQ, D_MODEL), 1.0),
        'x_sample': nrm(ks[1], (DEC_BATCH, DEC_SEQ, D_MODEL), 1.0),
        'cache_k': nrm(ks[2], (DEC_BATCH, PAST_LEN, FX_HEADS, FX_HEAD_DIM), 1.0),
        'cache_v': nrm(ks[3], (DEC_BATCH, PAST_LEN, FX_HEADS, FX_HEAD_DIM), 1.0),
        'cache_logf': jax.nn.log_sigmoid(2.5 + nrm(ks[4], (DEC_BATCH, PAST_LEN, FX_HEADS), 1.0)),
        'state_delta': nrm(ks[5], (N_A_LAYERS, DEC_BATCH, DN_V_HEADS, DN_HEAD_DIM, DN_HEAD_DIM),
                           DN_HEAD_DIM ** -0.5),
        'state_conv': nrm(ks[6], (N_A_LAYERS, DEC_BATCH, CONV_W - 1, CONV_DIM), 1.0),
        'meta_tokens': nrm(ks[7], (N_META, D_MODEL), 1.0),
        'a_norm_g': 1.0 + nrm(ks[8], (N_A_LAYERS, D_MODEL), 0.02),
        'a_w_in': nrm(ks[9], (N_A_LAYERS, D_MODEL, DN_IN), D_MODEL ** -0.5),
        'a_conv_w': nrm(ks[10], (N_A_LAYERS, CONV_W, CONV_DIM), CONV_W ** -0.5),
        'a_a_log': jnp.log(jax.random.uniform(ks[11], (N_A_LAYERS, DN_V_HEADS), f32, 1.0, 16.0)),
        'a_dt_bias': dt + jnp.log(-jnp.expm1(-dt)),
        'a_out_norm_g': 1.0 + nrm(ks[13], (N_A_LAYERS, DN_HEAD_DIM), 0.02),
        'a_w_out': nrm(ks[14], (N_A_LAYERS, DN_V_W, D_MODEL), DN_V_W ** -0.5),
        'kv_norm_g': 1.0 + nrm(ks[15], (D_MODEL,), 0.02),
        'kv_w': nrm(ks[16], (D_MODEL, KV_IN), D_MODEL ** -0.5),
        'kv_f_bias': jax.random.uniform(ks[17], (FX_HEADS,), f32, 1.0, 4.0),
        'kv_k_norm_g': 1.0 + nrm(ks[18], (FX_HEAD_DIM,), 0.02),
        'b_norm_g': 1.0 + nrm(ks[19], (N_B_LAYERS, D_MODEL), 0.02),
        'b_w_in': nrm(ks[20], (N_B_LAYERS, D_MODEL, 2 * FX_W), D_MODEL ** -0.5),
        'b_q_norm_g': 1.0 + nrm(ks[21], (N_B_LAYERS, FX_HEAD_DIM), 0.02),
        'b_w_out': nrm(ks[22], (N_B_LAYERS, FX_W, D_MODEL), FX_W ** -0.5),
    }


def reference(x_prompt, x_sample, cache_k, cache_v, cache_logf, state_delta, state_conv,
              meta_tokens, a_norm_g, a_w_in, a_conv_w, a_a_log, a_dt_bias, a_out_norm_g, a_w_out,
              kv_norm_g, kv_w, kv_f_bias, kv_k_norm_g, b_norm_g, b_w_in, b_q_norm_g, b_w_out):
    f32 = jnp.float32
    bp = x_prompt.shape[0]
    n_new = x_sample.shape[1]
    past = cache_k.shape[1]
    meta = jnp.broadcast_to(meta_tokens.astype(x_prompt.dtype)[None], (bp, N_META, D_MODEL))
    hp = jnp.concatenate([meta, x_prompt], axis=1)
    hs = x_sample
    conv0 = jnp.zeros((bp, CONV_W - 1, CONV_DIM), hp.dtype)
    s0 = jnp.zeros((bp, DN_V_HEADS, DN_HEAD_DIM, DN_HEAD_DIM), f32)
    q_pos_s = past + jnp.arange(n_new)
    k_pos_s = jnp.arange(past + n_new)
    delta_p, conv_p, delta_s, conv_s = [], [], [], []
    for layer in range(DEPTH):
        if layer < N_A_LAYERS:
            wa = (a_norm_g[layer], a_w_in[layer], a_conv_w[layer], a_a_log[layer],
                  a_dt_bias[layer], a_out_norm_g[layer], a_w_out[layer])
            hp, c_st, d_st = delta_layer(hp, conv0, s0, *wa)
            conv_p.append(c_st)
            delta_p.append(d_st)
            hs, c_st, d_st = delta_layer(hs, state_conv[layer], state_delta[layer].astype(f32), *wa)
            conv_s.append(c_st)
            delta_s.append(d_st)
        else:
            if layer == N_A_LAYERS:
                k_p, v_p, logf_p = kv_side(hp, kv_norm_g, kv_w, kv_f_bias, kv_k_norm_g)
                f_p = jnp.cumsum(logf_p, axis=1)
                k_s, v_s, logf_s = kv_side(hs, kv_norm_g, kv_w, kv_f_bias, kv_k_norm_g)
                k_all = jnp.concatenate([cache_k.astype(k_s.dtype), k_s], axis=1)
                v_all = jnp.concatenate([cache_v.astype(v_s.dtype), v_s], axis=1)
                f_all = jnp.cumsum(jnp.concatenate([cache_logf.astype(f32), logf_s], axis=1), axis=1)
                f_q_s = f_all[:, past:]
            j = layer - N_A_LAYERS
            q, z = fox_queries(hp, b_norm_g[j], b_w_in[j], b_q_norm_g[j])
            hp = fox_out(hp, fox_prompt_attend(q, k_p, v_p, f_p), z, b_w_out[j])
            q, z = fox_queries(hs, b_norm_g[j], b_w_in[j], b_q_norm_g[j])
            hs = fox_out(hs, fox_attend(q, k_all, v_all, f_q_s, f_all, q_pos_s, k_pos_s), z, b_w_out[j])
    y_prompt = hp[:, N_META:]
    return (y_prompt, hs,
            jnp.stack(delta_p).astype(x_prompt.dtype), jnp.stack(conv_p), k_p, v_p,
            logf_p.astype(x_prompt.dtype),
            jnp.stack(delta_s).astype(state_delta.dtype), jnp.stack(conv_s).astype(state_conv.dtype),
            k_s, v_s, logf_s.astype(cache_logf.dtype))
```

```python
import functools

import jax
import jax.numpy as jnp
from jax import lax
from jax.experimental import pallas as pl
from jax.experimental.pallas import tpu as pltpu

F32 = jnp.float32
BF16 = jnp.bfloat16

HEAD = 128
N_META = 16
CONV_W = 4
EPS = 1e-6
PROMPT_CHUNK = 64
META_BLOCK = 64
VMEM_LIMIT = 56 * 1024 * 1024

_CONTRACT_LAST = (((1,), (1,)), ((), ()))
_CONTRACT_FIRST = (((0,), (0,)), ((), ()))


def _pick(n, candidates):
    for c in candidates:
        if n % c == 0:
            return c
    raise ValueError(f"no tile in {candidates} divides {n}")


def _mm_kernel(x_ref, w_ref, o_ref):
    o_ref[...] = jnp.dot(x_ref[...], w_ref[...], preferred_element_type=F32).astype(o_ref.dtype)


def _mm_res_kernel(x_ref, w_ref, r_ref, o_ref):
    acc = jnp.dot(x_ref[...], w_ref[...], preferred_element_type=F32)
    o_ref[...] = (r_ref[...] + acc).astype(o_ref.dtype)


def _matmul(x, w, res=None, *, tm, name):
    t, k = x.shape
    n = w.shape[1]
    tn = _pick(n, (1024, 512, 256, 128))
    grid = (n // tn, t // tm)
    in_specs = [pl.BlockSpec((tm, k), lambda j, i: (i, 0)),
                pl.BlockSpec((k, tn), lambda j, i: (0, j))]
    args = [x, w]
    body = _mm_kernel
    if res is not None:
        in_specs.append(pl.BlockSpec((tm, tn), lambda j, i: (i, j)))
        args.append(res)
        body = _mm_res_kernel
    return pl.pallas_call(
        body,
        grid=grid,
        in_specs=in_specs,
        out_specs=pl.BlockSpec((tm, tn), lambda j, i: (i, j)),
        out_shape=jax.ShapeDtypeStruct((t, n), F32),
        compiler_params=pltpu.CompilerParams(
            dimension_semantics=("parallel", "parallel"), vmem_limit_bytes=VMEM_LIMIT),
        name=name,
    )(*args)


def _tri_inverse(m, row, col, c):
    eye = (row == col).astype(F32)
    same2 = (row >> 1) == (col >> 1)
    a = eye - jnp.where(same2, m, 0.0)
    shift = 1
    while (2 << shift) <= c:
        off = ((row >> (shift + 1)) == (col >> (shift + 1))) & ((row >> shift) != (col >> shift))
        ms = jnp.where(off, m, 0.0).astype(BF16)
        ab = a.astype(BF16)
        t = jnp.dot(ab, ms, preferred_element_type=F32)
        a = a - jnp.dot(t.astype(BF16), ab, preferred_element_type=F32)
        shift += 1
    return a


def _delta_prep_kernel(q_ref, k_ref, v_ref, gc_ref, bc_ref, gt_ref,
                       kq_ref, aq_ref, bv_ref, eg_ref, kb_ref, *, c, pg, hv):
    p = pl.program_id(2)
    row = lax.broadcasted_iota(jnp.int32, (c, c), 0)
    col = lax.broadcasted_iota(jnp.int32, (c, c), 1)
    incl = row >= col
    strict = row > col
    lane = lax.broadcasted_iota(jnp.int32, (c, hv), 1)
    gc_all = gc_ref[0]
    bc_all = bc_ref[0]
    for pp in range(pg):
        q = q_ref[0, :, pp * HEAD:(pp + 1) * HEAD]
        k = k_ref[0, :, pp * HEAD:(pp + 1) * HEAD]
        qb = q.astype(BF16)
        kb = k.astype(BF16)
        kb_ref[0, :, pp * HEAD:(pp + 1) * HEAD] = kb
        kk = lax.dot_general(kb, kb, _CONTRACT_LAST, preferred_element_type=F32)
        qk = lax.dot_general(qb, kb, _CONTRACT_LAST, preferred_element_type=F32)
        for hh in range(2):
            hl = 2 * pp + hh
            h = 2 * (p * pg + pp) + hh
            g_col = jnp.sum(jnp.where(lane == h, gc_all, 0.0), axis=1, keepdims=True)
            b_col = jnp.sum(jnp.where(lane == h, bc_all, 0.0), axis=1, keepdims=True)
            g_row = gt_ref[0, 0, pl.ds(h, 1), :]
            decay = jnp.where(incl, jnp.exp(jnp.where(incl, g_col - g_row, 0.0)), 0.0)
            m = jnp.where(strict, b_col * kk * decay, 0.0)
            a = _tri_inverse(m, row, col, c)
            e_g = jnp.exp(g_col)
            g_last = g_row[:, c - 1:c]
            e_tail = jnp.exp(g_last - g_col)
            pa = jnp.dot((qk * decay).astype(BF16), a.astype(BF16), preferred_element_type=F32)
            aq_ref[0, 0, hl, :c, :] = (e_tail * a).astype(BF16)
            aq_ref[0, 0, hl, c:, :] = pa.astype(BF16)
            kq_ref[0, 0, hl, :c, :] = (k * (b_col * e_g)).astype(BF16)
            kq_ref[0, 0, hl, c:, :] = (q * e_g).astype(BF16)
            bv_ref[0, :, hl * HEAD:(hl + 1) * HEAD] = v_ref[0, :, hl * HEAD:(hl + 1) * HEAD] * b_col
            eg_ref[0, 0, hl] = jnp.broadcast_to(jnp.exp(g_last), (8, HEAD))


def _delta_scan_kernel(kq_ref, aq_ref, bv_ref, eg_ref, kb_ref, s0_ref,
                       o_ref, sfin_ref, s_scr, *, c, ps):
    ci = pl.program_id(2)

    @pl.when(ci == 0)
    def _():
        s_scr[...] = s0_ref[0]

    for pp in range(ps):
        kb = kb_ref[0, :, pp * HEAD:(pp + 1) * HEAD]
        for hh in range(2):
            hl = 2 * pp + hh
            s = s_scr[hl]
            x = jnp.dot(kq_ref[0, 0, hl], s.astype(BF16), preferred_element_type=F32)
            r = bv_ref[0, :, hl * HEAD:(hl + 1) * HEAD] - x[:c]
            y = jnp.dot(aq_ref[0, 0, hl], r.astype(BF16), preferred_element_type=F32)
            o_ref[0, :, hl * HEAD:(hl + 1) * HEAD] = x[c:] + y[c:]
            upd = lax.dot_general(kb, y[:c].astype(BF16), _CONTRACT_FIRST,
                                  preferred_element_type=F32)
            s_scr[hl] = s * eg_ref[0, 0, hl, 0:1, :] + upd

    @pl.when(ci == pl.num_programs(2) - 1)
    def _():
        sfin_ref[0] = s_scr[...]


def _gated_delta_rule(q, k, v, beta, g, s0, *, c, first_block):
    b, l, qk_w = q.shape
    hv = beta.shape[-1]
    hk = qk_w // HEAD
    assert hv == 2 * hk
    nc = l // c
    pg = _pick(hk, (4, 2, 1))
    ps = _pick(hk, (8, 4, 2, 1))
    g_cum = jnp.cumsum(g.reshape(b, nc, c, hv), axis=2)
    g_col = g_cum.reshape(b, l, hv)
    g_t = jnp.swapaxes(g_cum, 2, 3)

    kq, aq, bv, eg, kb = pl.pallas_call(
        functools.partial(_delta_prep_kernel, c=c, pg=pg, hv=hv),
        grid=(b, nc, hk // pg),
        in_specs=[
            pl.BlockSpec((1, c, pg * HEAD), lambda i, j, p: (i, j, p)),
            pl.BlockSpec((1, c, pg * HEAD), lambda i, j, p: (i, j, p)),
            pl.BlockSpec((1, c, 2 * pg * HEAD), lambda i, j, p: (i, j, p)),
            pl.BlockSpec((1, c, hv), lambda i, j, p: (i, j, 0)),
            pl.BlockSpec((1, c, hv), lambda i, j, p: (i, j, 0)),
            pl.BlockSpec((1, 1, hv, c), lambda i, j, p: (i, j, 0, 0)),
        ],
        out_specs=[
            pl.BlockSpec((1, 1, 2 * pg, 2 * c, HEAD), lambda i, j, p: (i, j, p, 0, 0)),
            pl.BlockSpec((1, 1, 2 * pg, 2 * c, c), lambda i, j, p: (i, j, p, 0, 0)),
            pl.BlockSpec((1, c, 2 * pg * HEAD), lambda i, j, p: (i, j, p)),
            pl.BlockSpec((1, 1, 2 * pg, 8, HEAD), lambda i, j, p: (i, j, p, 0, 0)),
            pl.BlockSpec((1, c, pg * HEAD), lambda i, j, p: (i, j, p)),
        ],
        out_shape=[
            jax.ShapeDtypeStruct((b, nc, hv, 2 * c, HEAD), BF16),
            jax.ShapeDtypeStruct((b, nc, hv, 2 * c, c), BF16),
            jax.ShapeDtypeStruct((b, l, hv * HEAD), F32),
            jax.ShapeDtypeStruct((b, nc, hv, 8, HEAD), F32),
            jax.ShapeDtypeStruct((b, l, qk_w), BF16),
        ],
        compiler_params=pltpu.CompilerParams(
            dimension_semantics=("parallel", "parallel", "parallel"), vmem_limit_bytes=VMEM_LIMIT),
        name="delta_prep",
    )(q, k, v, g_col, beta, g_t)

    def blk(j):
        return (j + first_block) % nc

    o, s_fin = pl.pallas_call(
        functools.partial(_delta_scan_kernel, c=c, ps=ps),
        grid=(b, hk // ps, nc),
        in_specs=[
            pl.BlockSpec((1, 1, 2 * ps, 2 * c, HEAD), lambda i, p, j: (i, blk(j), p, 0, 0)),
            pl.BlockSpec((1, 1, 2 * ps, 2 * c, c), lambda i, p, j: (i, blk(j), p, 0, 0)),
            pl.BlockSpec((1, c, 2 * ps * HEAD), lambda i, p, j: (i, blk(j), p)),
            pl.BlockSpec((1, 1, 2 * ps, 8, HEAD), lambda i, p, j: (i, blk(j), p, 0, 0)),
            pl.BlockSpec((1, c, ps * HEAD), lambda i, p, j: (i, blk(j), p)),
            pl.BlockSpec((1, 2 * ps, HEAD, HEAD), lambda i, p, j: (i, p, 0, 0)),
        ],
        out_specs=[
            pl.BlockSpec((1, c, 2 * ps * HEAD), lambda i, p, j: (i, blk(j), p)),
            pl.BlockSpec((1, 2 * ps, HEAD, HEAD), lambda i, p, j: (i, p, 0, 0)),
        ],
        out_shape=[
            jax.ShapeDtypeStruct((b, l, hv * HEAD), F32),
            jax.ShapeDtypeStruct((b, hv, HEAD, HEAD), F32),
        ],
        scratch_shapes=[pltpu.VMEM((2 * ps, HEAD, HEAD), F32)],
        compiler_params=pltpu.CompilerParams(
            dimension_semantics=("parallel", "parallel", "arbitrary"), vmem_limit_bytes=VMEM_LIMIT),
        name="delta_scan",
    )(kq, aq, bv, eg, kb, s0)
    return o, s_fin


def _fox_prompt_kernel(q_ref, k_ref, v_ref, fq_ref, fkr_ref, fkm_ref, o_ref, *, seq, tq, nh):
    h = pl.program_id(1)
    scale = HEAD ** -0.5
    nq = seq // tq
    km = k_ref[0, seq:seq + META_BLOCK, :]
    vm = v_ref[0, seq:seq + META_BLOCK, :]
    fkm = fkm_ref[0, pl.ds(h, 1), :]
    lane_q = lax.broadcasted_iota(jnp.int32, (tq, nh), 1)
    col_m = lax.broadcasted_iota(jnp.int32, (tq, META_BLOCK), 1)
    row_d = lax.broadcasted_iota(jnp.int32, (tq, tq), 0)
    col_d = lax.broadcasted_iota(jnp.int32, (tq, tq), 1)

    def scores(qt, kt, fq, fk):
        s = lax.dot_general(qt, kt, _CONTRACT_LAST, preferred_element_type=F32)
        return s * scale + (fq - fk)

    def online(carry, s, vt):
        m, l, acc = carry
        m_new = jnp.maximum(m, jnp.max(s, axis=1, keepdims=True))
        alpha = jnp.exp(m - m_new)
        p = jnp.exp(s - m_new)
        l = alpha * l + jnp.sum(p, axis=1, keepdims=True)
        acc = alpha * acc + jnp.dot(p.astype(BF16), vt, preferred_element_type=F32)
        return m_new, l, acc

    for i in range(nq):
        qt = q_ref[0, i * tq:(i + 1) * tq, :]
        fq = jnp.sum(jnp.where(lane_q == h, fq_ref[0, i * tq:(i + 1) * tq, :], 0.0),
                     axis=1, keepdims=True)
        s = jnp.where(col_m < N_META, scores(qt, km, fq, fkm), -jnp.inf)
        m = jnp.max(s, axis=1, keepdims=True)
        p = jnp.exp(s - m)
        carry = (m, jnp.sum(p, axis=1, keepdims=True),
                 jnp.dot(p.astype(BF16), vm, preferred_element_type=F32))

        def body(j, carry, qt=qt, fq=fq):
            start = pl.multiple_of(j * tq, tq)
            kt = k_ref[0, pl.ds(start, tq), :]
            vt = v_ref[0, pl.ds(start, tq), :]
            fk = fkr_ref[0, j, pl.ds(h, 1), :]
            return online(carry, scores(qt, kt, fq, fk), vt)

        carry = lax.fori_loop(0, i, body, carry)
        kt = k_ref[0, i * tq:(i + 1) * tq, :]
        vt = v_ref[0, i * tq:(i + 1) * tq, :]
        fk = fkr_ref[0, i, pl.ds(h, 1), :]
        s = jnp.where(col_d <= row_d, scores(qt, kt, fq, fk), -jnp.inf)
        m, l, acc = online(carry, s, vt)
        o_ref[0, i * tq:(i + 1) * tq, :] = acc / l

    lane_m = lax.broadcasted_iota(jnp.int32, (META_BLOCK, nh), 1)
    row_m = lax.broadcasted_iota(jnp.int32, (META_BLOCK, META_BLOCK), 0)
    col_mm = lax.broadcasted_iota(jnp.int32, (META_BLOCK, META_BLOCK), 1)
    qm = q_ref[0, seq:seq + META_BLOCK, :]
    fqm = jnp.sum(jnp.where(lane_m == h, fq_ref[0, seq:seq + META_BLOCK, :], 0.0),
                  axis=1, keepdims=True)
    visible = (col_mm < N_META) & (col_mm <= row_m)
    s = jnp.where(visible, scores(qm, km, fqm, fkm), -jnp.inf)
    m = jnp.max(s, axis=1, keepdims=True)
    p = jnp.exp(s - m)
    o_ref[0, seq:seq + META_BLOCK, :] = (
        jnp.dot(p.astype(BF16), vm, preferred_element_type=F32) / jnp.sum(p, axis=1, keepdims=True))


def _fox_prompt(q, k, v, f, *, seq):
    b, l, w = q.shape
    nh = w // HEAD
    tq = _pick(seq, (512, 256, 128))
    f_real = jnp.swapaxes(f[:, :seq].reshape(b, seq // tq, tq, nh), 2, 3)
    f_meta = jnp.swapaxes(f[:, seq:seq + META_BLOCK], 1, 2)
    return pl.pallas_call(
        functools.partial(_fox_prompt_kernel, seq=seq, tq=tq, nh=nh),
        grid=(b, nh),
        in_specs=[
            pl.BlockSpec((1, l, HEAD), lambda i, h: (i, 0, h)),
            pl.BlockSpec((1, l, HEAD), lambda i, h: (i, 0, h)),
            pl.BlockSpec((1, l, HEAD), lambda i, h: (i, 0, h)),
            pl.BlockSpec((1, l, nh), lambda i, h: (i, 0, 0)),
            pl.BlockSpec((1, seq // tq, nh, tq), lambda i, h: (i, 0, 0, 0)),
            pl.BlockSpec((1, nh, META_BLOCK), lambda i, h: (i, 0, 0)),
        ],
        out_specs=pl.BlockSpec((1, l, HEAD), lambda i, h: (i, 0, h)),
        out_shape=jax.ShapeDtypeStruct((b, l, w), F32),
        compiler_params=pltpu.CompilerParams(
            dimension_semantics=("parallel", "parallel"), vmem_limit_bytes=VMEM_LIMIT),
        name="fox_prompt",
    )(q, k, v, f, f_real, f_meta)


def _fox_sample_kernel(q_ref, ck_ref, cv_ref, ks_ref, vs_ref, fq_ref, fkc_ref, fkn_ref,
                       o_ref, m_scr, l_scr, acc_scr, *, nh, n_new):
    j = pl.program_id(1)
    last = pl.num_programs(1) - 1
    scale = HEAD ** -0.5

    @pl.when(j == 0)
    def _():
        m_scr[...] = jnp.full(m_scr.shape, -jnp.inf, F32)
        l_scr[...] = jnp.zeros(l_scr.shape, F32)
        acc_scr[...] = jnp.zeros(acc_scr.shape, F32)

    def update(h, kt, vt, fk, mask):
        sl = slice(h * HEAD, (h + 1) * HEAD)
        s = lax.dot_general(q_ref[0, :, sl], kt, _CONTRACT_LAST, preferred_element_type=F32)
        s = s * scale + (fq_ref[0, :, h:h + 1] - fk)
        if mask is not None:
            s = jnp.where(mask, s, -jnp.inf)
        m_old = m_scr[:, sl]
        m_new = jnp.maximum(m_old, jnp.max(s, axis=1, keepdims=True))
        alpha = jnp.exp(m_old - m_new)
        p = jnp.exp(s - m_new[:, :1])
        l_scr[:, sl] = alpha * l_scr[:, sl] + jnp.sum(p, axis=1, keepdims=True)
        acc_scr[:, sl] = alpha * acc_scr[:, sl] + jnp.dot(p.astype(BF16), vt,
                                                        preferred_element_type=F32)
        m_scr[:, sl] = m_new

    @pl.when(j < last)
    def _():
        for h in range(nh):
            sl = slice(h * HEAD, (h + 1) * HEAD)
            update(h, ck_ref[0, :, sl].astype(BF16), cv_ref[0, :, sl].astype(BF16),
                   fkc_ref[0, 0, h:h + 1, :], None)

    @pl.when(j == last)
    def _():
        row = lax.broadcasted_iota(jnp.int32, (n_new, n_new), 0)
        col = lax.broadcasted_iota(jnp.int32, (n_new, n_new), 1)
        for h in range(nh):
            sl = slice(h * HEAD, (h + 1) * HEAD)
            update(h, ks_ref[0, :, sl], vs_ref[0, :, sl], fkn_ref[0, h:h + 1, :], col <= row)
        o_ref[0] = acc_scr[...] / l_scr[...]


def _fox_sample(q, cache_k, cache_v, k_new, v_new, f_cache, f_new):
    b, n_new, w = q.shape
    past = cache_k.shape[1]
    nh = w // HEAD
    tk = _pick(past, (512, 256, 128))
    nk = past // tk
    fkc = jnp.swapaxes(f_cache.reshape(b, nk, tk, nh), 2, 3)
    fkn = jnp.swapaxes(f_new, 1, 2)

    def cache_blk(j):
        return jnp.minimum(j, nk - 1)

    return pl.pallas_call(
        functools.partial(_fox_sample_kernel, nh=nh, n_new=n_new),
        grid=(b, nk + 1),
        in_specs=[
            pl.BlockSpec((1, n_new, w), lambda i, j: (i, 0, 0)),
            pl.BlockSpec((1, tk, w), lambda i, j: (i, cache_blk(j), 0)),
            pl.BlockSpec((1, tk, w), lambda i, j: (i, cache_blk(j), 0)),
            pl.BlockSpec((1, n_new, w), lambda i, j: (i, 0, 0)),
            pl.BlockSpec((1, n_new, w), lambda i, j: (i, 0, 0)),
            pl.BlockSpec((1, n_new, nh), lambda i, j: (i, 0, 0)),
            pl.BlockSpec((1, 1, nh, tk), lambda i, j: (i, cache_blk(j), 0, 0)),
            pl.BlockSpec((1, nh, n_new), lambda i, j: (i, 0, 0)),
        ],
        out_specs=pl.BlockSpec((1, n_new, w), lambda i, j: (i, 0, 0)),
        out_shape=jax.ShapeDtypeStruct((b, n_new, w), F32),
        scratch_shapes=[pltpu.VMEM((n_new, w), F32)] * 3,
        compiler_params=pltpu.CompilerParams(
            dimension_semantics=("parallel", "arbitrary"), vmem_limit_bytes=VMEM_LIMIT),
        name="fox_sample",
    )(q, cache_k, cache_v, k_new, v_new, f_new, fkc, fkn)


def _rms(x, g):
    return x * lax.rsqrt(jnp.mean(x * x, axis=-1, keepdims=True) + EPS) * g


def _rms_heads(x, g):
    lead = x.shape[:-1]
    xh = x.reshape(lead + (x.shape[-1] // HEAD, HEAD))
    return _rms(xh, g).reshape(x.shape)


def _l2_heads(x):
    lead = x.shape[:-1]
    xh = x.reshape(lead + (x.shape[-1] // HEAD, HEAD))
    return (xh * lax.rsqrt(jnp.sum(xh * xh, axis=-1, keepdims=True) + EPS)).reshape(x.shape)


def _short_conv_silu(xp, w, n):
    acc = xp[:, 0:n] * w[0]
    for i in range(1, CONV_W):
        acc = acc + xp[:, i:i + n] * w[i]
    return jax.nn.silu(acc)


def kernel(x_prompt, x_sample, cache_k, cache_v, cache_logf, state_delta, state_conv, meta_tokens,
           a_norm_g, a_w_in, a_conv_w, a_a_log, a_dt_bias, a_out_norm_g, a_w_out, kv_norm_g, kv_w,
           kv_f_bias, kv_k_norm_g, b_norm_g, b_w_in, b_q_norm_g, b_w_out):
    bp, seq, d = x_prompt.shape
    bs, ls, _ = x_sample.shape
    past, fxh = cache_k.shape[1], cache_k.shape[2]
    fx_w = fxh * HEAD
    n_a = a_w_in.shape[0]
    n_b = b_w_in.shape[0]
    hv = state_delta.shape[2]
    conv_dim = state_conv.shape[-1]
    v_w = hv * HEAD
    qk_w = (conv_dim - v_w) // 2
    assert seq % PROMPT_CHUNK == 0 and meta_tokens.shape[0] == N_META
    lp = seq + META_BLOCK
    n_p = bp * lp
    n_s = bs * ls
    tm = _pick(n_p + n_s, (1024, 512, 256, 128, 64))
    n_t = n_p + n_s

    meta = jnp.broadcast_to(meta_tokens[None], (bp, N_META, d))
    hp = jnp.concatenate([x_prompt, meta, jnp.zeros((bp, META_BLOCK - N_META, d), F32)], axis=1)
    h = jnp.concatenate([hp.reshape(n_p, d), x_sample.reshape(n_s, d)], axis=0)
    meta_valid = (jnp.arange(lp) < seq + N_META)[None, :, None]

    def split(u):
        return u[:n_p].reshape(bp, lp, -1), u[n_p:].reshape(bs, ls, -1)

    def to_logical(xp):
        return jnp.concatenate([xp[:, seq:seq + N_META], xp[:, :seq]], axis=1)

    def to_physical(xl):
        pad = jnp.zeros((bp, META_BLOCK - N_META) + xl.shape[2:], xl.dtype)
        return jnp.concatenate([xl[:, N_META:], xl[:, :N_META], pad], axis=1)

    delta_p, conv_p, delta_s, conv_s = [], [], [], []
    for layer in range(n_a):
        w_in = a_w_in[layer]
        n_main = conv_dim + v_w
        w_small = jnp.pad(w_in[:, n_main:], ((0, 0), (0, HEAD - 2 * hv)))
        xn = _rms(h, a_norm_g[layer]).astype(BF16)
        u = _matmul(xn, w_in[:, :n_main].astype(BF16), tm=tm, name="a_in_proj")
        ba = _matmul(xn, w_small.astype(BF16), tm=tm, name="a_gate_proj")
        u_p, u_s = split(u)
        ba_p, ba_s = split(ba)
        conv_w = a_conv_w[layer]
        neg_a = -jnp.exp(a_a_log[layer])
        dt_bias = a_dt_bias[layer]

        def gates(ba_x):
            beta = jax.nn.sigmoid(ba_x[..., :hv])
            g = neg_a * jax.nn.softplus(ba_x[..., hv:2 * hv] + dt_bias)
            return beta, g

        def qkv_of(c):
            q = _l2_heads(c[..., :qk_w]) * (HEAD ** -0.5)
            k = _l2_heads(c[..., qk_w:2 * qk_w])
            return q, k, c[..., 2 * qk_w:]

        x_log = to_logical(u_p[..., :conv_dim])
        xp = jnp.concatenate([jnp.zeros((bp, CONV_W - 1, conv_dim), F32), x_log], axis=1)
        conv_p.append(xp[:, xp.shape[1] - (CONV_W - 1):])
        c_phys = to_physical(_short_conv_silu(xp, conv_w, seq + N_META))
        q, k, v = qkv_of(c_phys)
        beta, g = gates(ba_p)
        beta = jnp.where(meta_valid, beta, 0.0)
        g = jnp.where(meta_valid, g, 0.0)
        o_p, s_p = _gated_delta_rule(q, k, v, beta, g, jnp.zeros((bp, hv, HEAD, HEAD), F32),
                                     c=PROMPT_CHUNK, first_block=seq // PROMPT_CHUNK)
        delta_p.append(s_p)

        xp = jnp.concatenate([state_conv[layer], u_s[..., :conv_dim]], axis=1)
        conv_s.append(xp[:, xp.shape[1] - (CONV_W - 1):])
        q, k, v = qkv_of(_short_conv_silu(xp, conv_w, ls))
        beta, g = gates(ba_s)
        o_s, s_s = _gated_delta_rule(q, k, v, beta, g, state_delta[layer].astype(F32),
                                     c=ls, first_block=0)
        delta_s.append(s_s)

        o = jnp.concatenate([o_p.reshape(n_p, v_w), o_s.reshape(n_s, v_w)], axis=0)
        z = u[:, conv_dim:]
        og = (_rms_heads(o, a_out_norm_g[layer]) * jax.nn.silu(z)).astype(BF16)
        h = _matmul(og, a_w_out[layer].astype(BF16), res=h, tm=tm // 2, name="a_out_proj")

    xn = _rms(h, kv_norm_g).astype(BF16)
    ukv = _matmul(xn, kv_w[:, :2 * fx_w].astype(BF16), tm=tm, name="kv_proj")
    w_f = jnp.pad(kv_w[:, 2 * fx_w:], ((0, 0), (0, HEAD - fxh)))
    uf = _matmul(xn, w_f.astype(BF16), tm=tm, name="kv_gate_proj")[:, :fxh]
    k_all = _rms_heads(ukv[:, :fx_w], kv_k_norm_g)
    v_all = ukv[:, fx_w:]
    logf = jax.nn.log_sigmoid(uf + kv_f_bias)
    k_pp, k_s = split(k_all)
    v_pp, v_s = split(v_all)
    logf_pp, logf_s = split(logf)
    logf_p = to_logical(logf_pp)
    f_p = to_physical(jnp.cumsum(logf_p, axis=1))
    f_all = jnp.cumsum(jnp.concatenate([cache_logf.astype(F32), logf_s], axis=1), axis=1)
    ck = cache_k.reshape(bs, past, fx_w)
    cv = cache_v.reshape(bs, past, fx_w)
    kb_p, vb_p = k_pp.astype(BF16), v_pp.astype(BF16)
    kb_s, vb_s = k_s.astype(BF16), v_s.astype(BF16)

    for j in range(n_b):
        xn = _rms(h, b_norm_g[j]).astype(BF16)
        u = _matmul(xn, b_w_in[j].astype(BF16), tm=tm, name="b_in_proj")
        qn = _rms_heads(u[:, :fx_w], b_q_norm_g[j]).astype(BF16)
        q_p, q_s = split(qn)
        o_p = _fox_prompt(q_p, kb_p, vb_p, f_p, seq=seq)
        o_s = _fox_sample(q_s, ck, cv, kb_s, vb_s, f_all[:, :past], f_all[:, past:])
        o = jnp.concatenate([o_p.reshape(n_p, fx_w), o_s.reshape(n_s, fx_w)], axis=0)
        og = (o * jax.nn.silu(u[:, fx_w:])).astype(BF16)
        h = _matmul(og, b_w_out[j].astype(BF16), res=h, tm=tm, name="b_out_proj")

    h_p, h_s = split(h)
    dt = x_prompt.dtype
    return (h_p[:, :seq], h_s,
            jnp.stack(delta_p).astype(dt), jnp.stack(conv_p),
            to_logical(k_pp).reshape(bp, seq + N_META, fxh, HEAD),
            to_logical(v_pp).reshape(bp, seq + N_META, fxh, HEAD),
            logf_p.astype(dt),
            jnp.stack(delta_s).astype(state_delta.dtype), jnp.stack(conv_s).astype(state_conv.dtype),
            k_s.reshape(bs, ls, fxh, HEAD), v_s.reshape(bs, ls, fxh, HEAD),
            logf_s.astype(cache_logf.dtype))
```
